```python
import math
import jax, jax.numpy as jnp
from jax import lax
import numpy as np

D_MODEL = 1024
BATCH = 2
SEQ = 16384
DEPTH = 1
DEC_BATCH = 128
DEC_SEQ = 4
PAST_LEN = 8192
PAGE_SIZE = 128

N_META = 16
Q_BLOCK = 128
ROPE_THETA = 10000.0
D_HEADS = 4
D_KV_HEADS = 2
D_HEAD_DIM = 64
F_HEADS = 8
F_KV_HEADS = 4
F_HEAD_DIM = 64
D_FF = 2816
LN_EPS = 1e-5
ALPHA = (2.0 * DEPTH) ** 0.25
BETA = (8.0 * DEPTH) ** -0.25
NEG_INF = -1e30

D_Q = D_HEADS * 2 * D_HEAD_DIM
D_K = D_KV_HEADS * 2 * D_HEAD_DIM
D_V = D_KV_HEADS * 2 * D_HEAD_DIM
F_Q = F_HEADS * F_HEAD_DIM
F_K = F_KV_HEADS * F_HEAD_DIM
F_V = F_KV_HEADS * F_HEAD_DIM
SPLIT_SIZES = (D_Q, D_K, D_V, F_Q, F_K, F_V, F_HEADS, D_MODEL, D_MODEL)
IN_WIDTH = sum(SPLIT_SIZES)

kernel_name = "hybrid_diffattn_fox_macaron_deepnorm_step"


def layer_norm(x, g, b):
    xf = x.astype(jnp.float32)
    mu = jnp.mean(xf, axis=-1, keepdims=True)
    var = jnp.mean(jnp.square(xf - mu), axis=-1, keepdims=True)
    y = (xf - mu) * lax.rsqrt(var + LN_EPS) * g.astype(jnp.float32) + b.astype(jnp.float32)
    return y.astype(x.dtype)


def rms_norm(x, g):
    return x * lax.rsqrt(jnp.mean(jnp.square(x), axis=-1, keepdims=True) + LN_EPS) * g.astype(jnp.float32)


def swiglu(x, w_in, w_out):
    a, u = jnp.split(x @ w_in, 2, axis=-1)
    return (jax.nn.silu(a) * u) @ w_out


def rope(x, pos):
    half = x.shape[-1] // 2
    inv_freq = ROPE_THETA ** (-jnp.arange(half, dtype=jnp.float32) / half)
    ang = pos.astype(jnp.float32)[:, None] * inv_freq[None, :]
    shape = (pos.shape[0],) + (1,) * (x.ndim - 3) + (half,)
    cos = jnp.cos(ang).reshape(shape)
    sin = jnp.sin(ang).reshape(shape)
    xf = x.astype(jnp.float32)
    x1, x2 = xf[..., :half], xf[..., half:]
    return jnp.concatenate([x1 * cos - x2 * sin, x2 * cos + x1 * sin], axis=-1).astype(x.dtype)


def causal_mask(q_pos, k_pos):
    return k_pos[None, :] <= q_pos[:, None]


def split_in(z):
    out = []
    start = 0
    for size in SPLIT_SIZES:
        out.append(z[..., start:start + size])
        start += size
    return out


def project_mixer_inputs(h, pos, w_in, b_f):
    n, t = h.shape[:2]
    dq, dk, dv, fq, fk, fv, fl, ga, gb = split_in(h @ w_in)
    dq = rope(dq.reshape(n, t, D_HEADS, 2, D_HEAD_DIM), pos)
    dk = rope(dk.reshape(n, t, D_KV_HEADS, 2, D_HEAD_DIM), pos)
    dv = dv.reshape(n, t, D_KV_HEADS, 2 * D_HEAD_DIM)
    fq = fq.reshape(n, t, F_HEADS, F_HEAD_DIM)
    fk = fk.reshape(n, t, F_KV_HEADS, F_HEAD_DIM)
    fv = fv.reshape(n, t, F_KV_HEADS, F_HEAD_DIM)
    logf = jax.nn.log_sigmoid(fl.astype(jnp.float32) + b_f.astype(jnp.float32))
    return dq, dk, dv, fq, fk, fv, logf, ga, gb


def diff_lambda(lq1, lk1, lq2, lk2, lambda_init):
    f32 = jnp.float32
    return (jnp.exp(jnp.sum(lq1.astype(f32) * lk1.astype(f32)))
            - jnp.exp(jnp.sum(lq2.astype(f32) * lk2.astype(f32))) + lambda_init)


def diff_attention(q, k, v, q_pos, k_pos, lam, norm_g, lambda_init):
    n, nq = q.shape[:2]
    rep = D_HEADS // D_KV_HEADS
    qg = q.reshape(n, nq, D_KV_HEADS, rep, 2, D_HEAD_DIM).astype(jnp.float32)
    s = jnp.einsum('nqgrmd,nkgmd->ngrmqk', qg, k.astype(jnp.float32)) * (D_HEAD_DIM ** -0.5)
    s = jnp.where(causal_mask(q_pos, k_pos), s, NEG_INF)
    p = jax.nn.softmax(s, axis=-1)
    a = p[:, :, :, 0] - lam * p[:, :, :, 1]
    o = jnp.einsum('ngrqk,nkgv->nqgrv', a, v.astype(jnp.float32))
    o = rms_norm(o, norm_g) * (1.0 - lambda_init)
    return o.reshape(n, nq, D_Q).astype(q.dtype)


def fox_attention(q, k, v, cq, ck, q_pos, k_pos):
    n, nq = q.shape[:2]
    nk = k.shape[1]
    rep = F_HEADS // F_KV_HEADS
    qg = q.reshape(n, nq, F_KV_HEADS, rep, F_HEAD_DIM).astype(jnp.float32)
    s = jnp.einsum('nqgrd,nkgd->ngrqk', qg, k.astype(jnp.float32)) * (F_HEAD_DIM ** -0.5)
    cqg = cq.reshape(n, nq, F_KV_HEADS, rep).transpose(0, 2, 3, 1)
    ckg = ck.reshape(n, nk, F_KV_HEADS, rep).transpose(0, 2, 3, 1)
    s = s + cqg[..., :, None] - ckg[..., None, :]
    s = jnp.where(causal_mask(q_pos, k_pos), s, NEG_INF)
    p = jax.nn.softmax(s, axis=-1)
    o = jnp.einsum('ngrqk,nkgd->nqgrd', p, v.astype(jnp.float32))
    return o.reshape(n, nq, F_Q).astype(q.dtype)


def merge_branches(y_diff, y_fox, ga, gb, w_up_diff, w_up_fox, w_o):
    m = jax.nn.sigmoid(ga) * (y_diff @ w_up_diff) + jax.nn.sigmoid(gb) * (y_fox @ w_up_fox)
    return m @ w_o


def setup_inputs(seed: int = 0) -> dict:
    key = jax.random.key(seed)
    ks = jax.random.split(key, 32)
    f32 = jnp.float32
    n_pages = PAST_LEN // PAGE_SIZE
    n_used = DEC_BATCH * n_pages
    n_pool = n_used + n_used // 4 + 1
    page_table = jax.random.permutation(ks[0], n_pool)[:n_used].reshape(DEC_BATCH, n_pages).astype(jnp.int32)

    def nrm(k, shape, scale):
        return jax.random.normal(k, shape, f32) * scale

    x_prompt = nrm(ks[1], (BATCH, SEQ, D_MODEL), 1.0)
    x_sample = nrm(ks[2], (DEC_BATCH, DEC_SEQ, D_MODEL), 1.0)
    cache_dk = nrm(ks[3], (DEPTH, n_pool, PAGE_SIZE, D_KV_HEADS, 2 * D_HEAD_DIM), 1.0)
    cache_dv = nrm(ks[4], (DEPTH, n_pool, PAGE_SIZE, D_KV_HEADS, 2 * D_HEAD_DIM), 1.0)
    cache_fk = nrm(ks[5], (DEPTH, n_pool, PAGE_SIZE, F_KV_HEADS, F_HEAD_DIM), 1.0)
    cache_fv = nrm(ks[6], (DEPTH, n_pool, PAGE_SIZE, F_KV_HEADS, F_HEAD_DIM), 1.0)
    cache_flogf = jax.nn.log_sigmoid(
        jax.random.uniform(ks[7], (DEPTH, n_pool, PAGE_SIZE, F_HEADS), f32, 1.0, 6.0)
        + nrm(ks[8], (DEPTH, n_pool, PAGE_SIZE, F_HEADS), 1.0))
    meta = nrm(ks[9], (N_META, D_MODEL), 1.0)
    ffn1_w_in = nrm(ks[10], (DEPTH, D_MODEL, 2 * D_FF), D_MODEL ** -0.5)
    ffn1_w_out = nrm(ks[11], (DEPTH, D_FF, D_MODEL), BETA * D_FF ** -0.5)
    ln1_g = 1.0 + nrm(ks[12], (DEPTH, D_MODEL), 0.02)
    ln1_b = nrm(ks[13], (DEPTH, D_MODEL), 0.02)
    w_in = nrm(ks[14], (DEPTH, D_MODEL, IN_WIDTH), D_MODEL ** -0.5)
    b_f = jax.random.uniform(ks[15], (DEPTH, F_HEADS), f32, 1.0, 6.0)
    lam_q1 = nrm(ks[16], (DEPTH, D_HEAD_DIM), 0.1)
    lam_k1 = nrm(ks[17], (DEPTH, D_HEAD_DIM), 0.1)
    lam_q2 = nrm(ks[18], (DEPTH, D_HEAD_DIM), 0.1)
    lam_k2 = nrm(ks[19], (DEPTH, D_HEAD_DIM), 0.1)
    diff_norm_g = 1.0 + nrm(ks[20], (DEPTH, 2 * D_HEAD_DIM), 0.02)
    w_up_diff = nrm(ks[21], (DEPTH, D_Q, D_MODEL), D_Q ** -0.5)
    w_up_fox = nrm(ks[22], (DEPTH, F_Q, D_MODEL), F_Q ** -0.5)
    w_o = nrm(ks[23], (DEPTH, D_MODEL, D_MODEL), BETA * D_MODEL ** -0.5)
    ln2_g = 1.0 + nrm(ks[24], (DEPTH, D_MODEL), 0.02)
    ln2_b = nrm(ks[25], (DEPTH, D_MODEL), 0.02)
    ffn2_w_in = nrm(ks[26], (DEPTH, D_MODEL, 2 * D_FF), D_MODEL ** -0.5)
    ffn2_w_out = nrm(ks[27], (DEPTH, D_FF, D_MODEL), BETA * D_FF ** -0.5)
    ln3_g = 1.0 + nrm(ks[28], (DEPTH, D_MODEL), 0.02)
    ln3_b = nrm(ks[29], (DEPTH, D_MODEL), 0.02)
    return {"x_prompt": x_prompt, "x_sample": x_sample,
            "cache_dk": cache_dk, "cache_dv": cache_dv, "cache_fk": cache_fk, "cache_fv": cache_fv,
            "cache_flogf": cache_flogf, "page_table": page_table, "meta": meta,
            "ffn1_w_in": ffn1_w_in, "ffn1_w_out": ffn1_w_out, "ln1_g": ln1_g, "ln1_b": ln1_b,
            "w_in": w_in, "b_f": b_f, "lam_q1": lam_q1, "lam_k1": lam_k1, "lam_q2": lam_q2,
            "lam_k2": lam_k2, "diff_norm_g": diff_norm_g, "w_up_diff": w_up_diff,
            "w_up_fox": w_up_fox, "w_o": w_o, "ln2_g": ln2_g, "ln2_b": ln2_b,
            "ffn2_w_in": ffn2_w_in, "ffn2_w_out": ffn2_w_out, "ln3_g": ln3_g, "ln3_b": ln3_b}


def reference(x_prompt, x_sample, cache_dk, cache_dv, cache_fk, cache_fv, cache_flogf, page_table, meta,
              ffn1_w_in, ffn1_w_out, ln1_g, ln1_b, w_in, b_f, lam_q1, lam_k1, lam_q2, lam_k2,
              diff_norm_g, w_up_diff, w_up_fox, w_o, ln2_g, ln2_b, ffn2_w_in, ffn2_w_out, ln3_g, ln3_b):
    b, seq = x_prompt.shape[:2]
    db, ds = x_sample.shape[:2]
    n_pages = PAST_LEN // PAGE_SIZE
    total = N_META + seq
    n_blk = seq // Q_BLOCK
    pos_p = jnp.arange(total)
    pos_s = PAST_LEN + jnp.arange(ds)
    k_pos_s = jnp.arange(PAST_LEN + ds)

    hp = jnp.concatenate([jnp.broadcast_to(meta[None].astype(x_prompt.dtype), (b, N_META, D_MODEL)),
                          x_prompt], axis=1)
    hs = x_sample
    p_dk, p_dv, p_fk, p_fv, p_lf = [], [], [], [], []
    s_dk, s_dv, s_fk, s_fv, s_lf = [], [], [], [], []

    def gather(cache, l):
        rows = cache[l, page_table]
        return rows.reshape((db, n_pages * PAGE_SIZE) + cache.shape[3:])

    for l in range(DEPTH):
        lambda_init = 0.8 - 0.6 * math.exp(-0.3 * l)
        lam = diff_lambda(lam_q1[l], lam_k1[l], lam_q2[l], lam_k2[l], lambda_init)
        keep_meta = l < DEPTH - 1

        hp = layer_norm(ALPHA * hp + 0.5 * swiglu(hp, ffn1_w_in[l], ffn1_w_out[l]), ln1_g[l], ln1_b[l])
        dq, dk, dv, fq, fk, fv, logf, ga, gb = project_mixer_inputs(hp, pos_p, w_in[l], b_f[l])
        cum = jnp.cumsum(logf, axis=1)

        def to_blocks(a):
            r = a[:, N_META:]
            return r.reshape((b, n_blk, Q_BLOCK) + r.shape[2:]).swapaxes(0, 1)

        def block_fn(args):
            i, qd, qf, cq = args
            q_pos = N_META + i * Q_BLOCK + jnp.arange(Q_BLOCK)
            yd_b = diff_attention(qd, dk, dv, q_pos, pos_p, lam, diff_norm_g[l], lambda_init)
            yf_b = fox_attention(qf, fk, fv, cq, cum, q_pos, pos_p)
            return yd_b, yf_b

        yd, yf = lax.map(block_fn, (jnp.arange(n_blk), to_blocks(dq), to_blocks(fq), to_blocks(cum)))
        yd = yd.swapaxes(0, 1).reshape(b, seq, D_Q)
        yf = yf.swapaxes(0, 1).reshape(b, seq, F_Q)
        if keep_meta:
            mpos = pos_p[:N_META]
            yd_m = diff_attention(dq[:, :N_META], dk[:, :N_META], dv[:, :N_META], mpos, mpos,
                                  lam, diff_norm_g[l], lambda_init)
            yf_m = fox_attention(fq[:, :N_META], fk[:, :N_META], fv[:, :N_META],
                                 cum[:, :N_META], cum[:, :N_META], mpos, mpos)
            yd = jnp.concatenate([yd_m, yd], axis=1)
            yf = jnp.concatenate([yf_m, yf], axis=1)
            res, ga_r, gb_r = hp, ga, gb
        else:
            res, ga_r, gb_r = hp[:, N_META:], ga[:, N_META:], gb[:, N_META:]
        mix = merge_branches(yd, yf, ga_r, gb_r, w_up_diff[l], w_up_fox[l], w_o[l])
        hp = layer_norm(ALPHA * res + mix, ln2_g[l], ln2_b[l])
        hp = layer_norm(ALPHA * hp + 0.5 * swiglu(hp, ffn2_w_in[l], ffn2_w_out[l]), ln3_g[l], ln3_b[l])
        p_dk.append(dk.reshape(b, total, D_KV_HEADS, 2 * D_HEAD_DIM))
        p_dv.append(dv)
        p_fk.append(fk)
        p_fv.append(fv)
        p_lf.append(logf)

        hs = layer_norm(ALPHA * hs + 0.5 * swiglu(hs, ffn1_w_in[l], ffn1_w_out[l]), ln1_g[l], ln1_b[l])
        sdq, sdk, sdv, sfq, sfk, sfv, slogf, sga, sgb = project_mixer_inputs(hs, pos_s, w_in[l], b_f[l])
        k_d = jnp.concatenate([gather(cache_dk, l).reshape(db, PAST_LEN, D_KV_HEADS, 2, D_HEAD_DIM),
                               sdk.astype(cache_dk.dtype)], axis=1)
        v_d = jnp.concatenate([gather(cache_dv, l), sdv.astype(cache_dv.dtype)], axis=1)
        k_f = jnp.concatenate([gather(cache_fk, l), sfk.astype(cache_fk.dtype)], axis=1)
        v_f = jnp.concatenate([gather(cache_fv, l), sfv.astype(cache_fv.dtype)], axis=1)
        c_past = jnp.cumsum(gather(cache_flogf, l).astype(jnp.float32), axis=1)
        c_new = c_past[:, -1:] + jnp.cumsum(slogf, axis=1)
        c_all = jnp.concatenate([c_past, c_new], axis=1)
        yd_s = diff_attention(sdq, k_d, v_d, pos_s, k_pos_s, lam, diff_norm_g[l], lambda_init)
        yf_s = fox_attention(sfq, k_f, v_f, c_new, c_all, pos_s, k_pos_s)
        mix_s = merge_branches(yd_s, yf_s, sga, sgb, w_up_diff[l], w_up_fox[l], w_o[l])
        hs = layer_norm(ALPHA * hs + mix_s, ln2_g[l], ln2_b[l])
        hs = layer_norm(ALPHA * hs + 0.5 * swiglu(hs, ffn2_w_in[l], ffn2_w_out[l]), ln3_g[l], ln3_b[l])
        s_dk.append(sdk.reshape(db, ds, D_KV_HEADS, 2 * D_HEAD_DIM))
        s_dv.append(sdv)
        s_fk.append(sfk)
        s_fv.append(sfv)
        s_lf.append(slogf)

    y_prompt = hp
    y_sample = hs
    return (y_prompt, y_sample,
            jnp.stack(p_dk), jnp.stack(p_dv), jnp.stack(p_fk), jnp.stack(p_fv), jnp.stack(p_lf),
            jnp.stack(s_dk), jnp.stack(s_dv), jnp.stack(s_fk), jnp.stack(s_fv), jnp.stack(s_lf))
```

```python
import functools
import math

import numpy as np
import jax
import jax.numpy as jnp
from jax import lax
from jax.experimental import pallas as pl
from jax.experimental.pallas import tpu as pltpu

N_META = 16
ROPE_THETA = 10000.0
D_HEADS = 4
D_KV_HEADS = 2
D_HEAD_DIM = 64
F_HEADS = 8
F_KV_HEADS = 4
F_HEAD_DIM = 64
LN_EPS = 1e-5
NEG_INF = -1e30

D_REP = D_HEADS // D_KV_HEADS
F_REP = F_HEADS // F_KV_HEADS
D_Q = D_HEADS * 2 * D_HEAD_DIM
D_K = D_KV_HEADS * 2 * D_HEAD_DIM
F_Q = F_HEADS * F_HEAD_DIM
F_K = F_KV_HEADS * F_HEAD_DIM

LANES = 128
LOG2E = 1.4426950408889634
QSCALE = (D_HEAD_DIM ** -0.5) * LOG2E
VMEM_LIMIT = 56 * 1024 * 1024

ROW_TILE = 512
SMALL_TILE = 128
ATTN_TILE = 512
DECODE_CHUNK = 8

AUG_C = F_HEAD_DIM
AUG_R0 = F_HEAD_DIM + 3
AUG_R1 = F_HEAD_DIM + 6

f32 = jnp.float32
bf16 = jnp.bfloat16


def _const_spec(shape):
    return pl.BlockSpec(shape, lambda *_: (0,) * len(shape), pipeline_mode=pl.Buffered(1))


def _dot(a, b):
    return jnp.dot(a, b, preferred_element_type=f32)


def _dot_nt(a, b):
    return lax.dot_general(a, b, (((1,), (1,)), ((), ())), preferred_element_type=f32)


def _layer_norm(y, g, b):
    mu = jnp.mean(y, axis=-1, keepdims=True)
    d = y - mu
    var = jnp.mean(d * d, axis=-1, keepdims=True)
    return d * lax.rsqrt(var + LN_EPS) * g + b


def _split3(x):
    hi = x.astype(bf16)
    r1 = x - hi.astype(f32)
    mid = r1.astype(bf16)
    lo = (r1 - mid.astype(f32)).astype(bf16)
    return hi, mid, lo


def _ffn_ln_kernel(x_ref, w_in_ref, w_out_ref, g_ref, b_ref, o_ref, *, alpha, d_ff, n_chunks):
    x = x_ref[...]
    xb = x.astype(bf16)
    ch = d_ff // n_chunks
    ffn = None
    for c in range(n_chunks):
        a = _dot(xb, w_in_ref[:, c * ch:(c + 1) * ch])
        u = _dot(xb, w_in_ref[:, d_ff + c * ch:d_ff + (c + 1) * ch])
        act = (a * jax.nn.sigmoid(a) * u).astype(bf16)
        part = _dot(act, w_out_ref[c * ch:(c + 1) * ch, :])
        ffn = part if ffn is None else ffn + part
    o_ref[...] = _layer_norm(alpha * x + 0.5 * ffn, g_ref[...], b_ref[...])


def _ffn_ln(x, w_in, w_out, g, b, *, alpha, tile):
    rows, d = x.shape
    d_ff = w_out.shape[0]
    n_chunks = 2 if (d_ff // 2) % LANES == 0 else 1
    kern = functools.partial(_ffn_ln_kernel, alpha=alpha, d_ff=d_ff, n_chunks=n_chunks)
    return pl.pallas_call(
        kern,
        out_shape=jax.ShapeDtypeStruct((rows, d), f32),
        grid=(rows // tile,),
        in_specs=[pl.BlockSpec((tile, d), lambda i: (i, 0)),
                  _const_spec(w_in.shape), _const_spec(w_out.shape),
                  _const_spec(g.shape), _const_spec(b.shape)],
        out_specs=pl.BlockSpec((tile, d), lambda i: (i, 0)),
        compiler_params=pltpu.CompilerParams(dimension_semantics=("arbitrary",),
                                             vmem_limit_bytes=VMEM_LIMIT),
        name="ffn_ln",
    )(x, w_in, w_out, g, b)


def _inproj_kernel(h_ref, cos_ref, sin_ref, w_ref, bf_ref,
                   qd_ref, kd32_ref, vd32_ref, kf32_ref, vf32_ref, lf_ref, lfw_ref,
                   kd16_ref, vd16_ref, qf_ref, kf16_ref, vf16_ref):
    hb = h_ref[...].astype(bf16)
    cos = cos_ref[...]
    sin = sin_ref[...]
    lane = lax.broadcasted_iota(jnp.int32, cos.shape, 1)
    first_half = (lane % D_HEAD_DIM) < (D_HEAD_DIM // 2)
    lower = lane < D_HEAD_DIM

    def seg(a, b):
        return _dot(hb, w_ref[:, a:b])

    def rope(x):
        rot = jnp.where(first_half, pltpu.roll(x, LANES - D_HEAD_DIM // 2, 1),
                        pltpu.roll(x, D_HEAD_DIM // 2, 1))
        return x * cos + rot * sin

    o = 0
    dq = seg(o, o + D_Q)
    o += D_Q
    for h in range(D_HEADS):
        r = rope(dq[:, h * LANES:(h + 1) * LANES]) * QSCALE
        qd_ref[2 * h] = jnp.where(lower, r, 0.0).astype(bf16)
        qd_ref[2 * h + 1] = jnp.where(lower, 0.0, r).astype(bf16)
    dk = seg(o, o + D_K)
    o += D_K
    for g in range(D_KV_HEADS):
        r = rope(dk[:, g * LANES:(g + 1) * LANES])
        kd32_ref[:, g * LANES:(g + 1) * LANES] = r
        kd16_ref[g] = r.astype(bf16)
    dv = seg(o, o + D_K)
    o += D_K
    vd32_ref[...] = dv
    for g in range(D_KV_HEADS):
        vd16_ref[g] = dv[:, g * LANES:(g + 1) * LANES].astype(bf16)
    fq = seg(o, o + F_Q)
    o += F_Q
    qf_ref[...] = (fq * QSCALE).astype(bf16)
    fk = seg(o, o + F_K)
    o += F_K
    kf32_ref[...] = fk
    kf16_ref[...] = fk.astype(bf16)
    fv = seg(o, o + F_K)
    o += F_K
    vf32_ref[...] = fv
    for g in range(F_KV_HEADS):
        vf16_ref[g] = fv[:, g * F_HEAD_DIM:(g + 1) * F_HEAD_DIM].astype(bf16)
    fl = seg(o, o + LANES) + bf_ref[...]
    lf = jnp.minimum(fl, 0.0) - jnp.log1p(jnp.exp(-jnp.abs(fl)))
    lfw_ref[...] = jnp.where(lane < F_HEADS, lf, 0.0)
    lf_ref[...] = lf[:, :F_HEADS]


def _inproj(h, cos_t, sin_t, w, b_f, *, tile, table_tiles):
    rows, d = h.shape
    n_w = w.shape[1]
    row = lambda i: (i, 0)
    tab = lambda i: (i % table_tiles, 0)
    grp = lambda i: (0, i, 0)
    out_shape = [
        jax.ShapeDtypeStruct((2 * D_HEADS, rows, LANES), bf16),
        jax.ShapeDtypeStruct((rows, D_K), f32),
        jax.ShapeDtypeStruct((rows, D_K), f32),
        jax.ShapeDtypeStruct((rows, F_K), f32),
        jax.ShapeDtypeStruct((rows, F_K), f32),
        jax.ShapeDtypeStruct((rows, F_HEADS), f32),
        jax.ShapeDtypeStruct((rows, LANES), f32),
        jax.ShapeDtypeStruct((D_KV_HEADS, rows, LANES), bf16),
        jax.ShapeDtypeStruct((D_KV_HEADS, rows, LANES), bf16),
        jax.ShapeDtypeStruct((rows, F_Q), bf16),
        jax.ShapeDtypeStruct((rows, F_K), bf16),
        jax.ShapeDtypeStruct((F_KV_HEADS, rows, F_HEAD_DIM), bf16),
    ]
    out_specs = [
        pl.BlockSpec((2 * D_HEADS, tile, LANES), grp),
        pl.BlockSpec((tile, D_K), row), pl.BlockSpec((tile, D_K), row),
        pl.BlockSpec((tile, F_K), row), pl.BlockSpec((tile, F_K), row),
        pl.BlockSpec((tile, F_HEADS), row), pl.BlockSpec((tile, LANES), row),
        pl.BlockSpec((D_KV_HEADS, tile, LANES), grp), pl.BlockSpec((D_KV_HEADS, tile, LANES), grp),
        pl.BlockSpec((tile, F_Q), row), pl.BlockSpec((tile, F_K), row),
        pl.BlockSpec((F_KV_HEADS, tile, F_HEAD_DIM), grp),
    ]
    return pl.pallas_call(
        _inproj_kernel,
        out_shape=out_shape,
        grid=(rows // tile,),
        in_specs=[pl.BlockSpec((tile, d), row), pl.BlockSpec((tile, LANES), tab),
                  pl.BlockSpec((tile, LANES), tab), _const_spec((d, n_w)), _const_spec((1, LANES))],
        out_specs=out_specs,
        compiler_params=pltpu.CompilerParams(dimension_semantics=("arbitrary",),
                                             vmem_limit_bytes=VMEM_LIMIT),
        name="inproj",
    )(h, cos_t, sin_t, w, b_f)


def _fox_aug_kernel(lfw_ref, qf_ref, kf_ref, tri_ref, pq_ref, pk_ref, oq_ref, ok_ref,
                    qa_ref, ka_ref, carry_ref, *, tiles_per_seq, ref_row):
    i = pl.program_id(0)

    @pl.when(i % tiles_per_seq == 0)
    def _():
        carry_ref[...] = jnp.zeros_like(carry_ref)

    tri = tri_ref[...]
    hi, mid, lo = _split3(lfw_ref[...])
    c = carry_ref[...] + (_dot(tri, hi) + _dot(tri, mid) + _dot(tri, lo))
    carry_ref[...] = c[-1:, :]
    if ref_row is not None:
        c = c - c[ref_row:ref_row + 1, :]
    c3 = jnp.concatenate(_split3(c * LOG2E), axis=1)
    q_aug = _dot(c3, pq_ref[...]) + oq_ref[...]
    k_aug = _dot(c3, pk_ref[...]) + ok_ref[...]
    lane = lax.broadcasted_iota(jnp.int32, (c.shape[0], LANES), 1)
    lower = lane < F_HEAD_DIM

    def head_cols(ref, h):
        slab = ref[:, (h // 2) * LANES:(h // 2 + 1) * LANES].astype(f32)
        if h % 2:
            slab = pltpu.roll(slab, F_HEAD_DIM, 1)
        return jnp.where(lower, slab, 0.0)

    for h in range(F_HEADS):
        qa_ref[h] = (head_cols(qf_ref, h) + q_aug[:, h * LANES:(h + 1) * LANES]).astype(bf16)
    for g in range(F_KV_HEADS):
        ka_ref[g] = (head_cols(kf_ref, g) + k_aug[:, g * LANES:(g + 1) * LANES]).astype(bf16)


def _aug_constants(tile):
    tri = np.tril(np.ones((tile, tile), np.float32))
    pq = np.zeros((3 * LANES, F_HEADS * LANES), np.float32)
    pk = np.zeros((3 * LANES, F_KV_HEADS * LANES), np.float32)
    oq = np.zeros((1, F_HEADS * LANES), np.float32)
    ok = np.zeros((1, F_KV_HEADS * LANES), np.float32)
    for h in range(F_HEADS):
        g, r = divmod(h, F_REP)
        for part in range(3):
            pq[part * LANES + h, h * LANES + AUG_C + part] = 1.0
            pk[part * LANES + h, g * LANES + (AUG_R0, AUG_R1)[r] + part] = -1.0
            oq[0, h * LANES + (AUG_R0, AUG_R1)[r] + part] = 1.0
    for g in range(F_KV_HEADS):
        ok[0, g * LANES + AUG_C:g * LANES + AUG_C + 3] = 1.0
    return (jnp.asarray(tri, bf16), jnp.asarray(pq, bf16), jnp.asarray(pk, bf16),
            jnp.asarray(oq, f32), jnp.asarray(ok, f32))


def _fox_aug(lfw, qf, kf16, *, rows, tile, tiles_per_seq, ref_row):
    tri, pq, pk, oq, ok = _aug_constants(tile)
    row = lambda i: (i, 0)
    grp = lambda i: (0, i, 0)
    kern = functools.partial(_fox_aug_kernel, tiles_per_seq=tiles_per_seq, ref_row=ref_row)
    return pl.pallas_call(
        kern,
        out_shape=[jax.ShapeDtypeStruct((F_HEADS, rows, LANES), bf16),
                   jax.ShapeDtypeStruct((F_KV_HEADS, rows, LANES), bf16)],
        grid=(rows // tile,),
        in_specs=[pl.BlockSpec((tile, LANES), row), pl.BlockSpec((tile, F_Q), row),
                  pl.BlockSpec((tile, F_K), row),
                  _const_spec(tri.shape), _const_spec(pq.shape), _const_spec(pk.shape),
                  _const_spec(oq.shape), _const_spec(ok.shape)],
        out_specs=[pl.BlockSpec((F_HEADS, tile, LANES), grp),
                   pl.BlockSpec((F_KV_HEADS, tile, LANES), grp)],
        scratch_shapes=[pltpu.VMEM((1, LANES), f32)],
        compiler_params=pltpu.CompilerParams(dimension_semantics=("arbitrary",),
                                             vmem_limit_bytes=VMEM_LIMIT),
        name="fox_aug",
    )(lfw, qf, kf16, tri, pq, pk, oq, ok)


def _diff_lambda(lam_ref, lambda_init):
    lp = lam_ref[...]
    s1 = jnp.sum(lp[0:1] * lp[1:2], axis=-1, keepdims=True)
    s2 = jnp.sum(lp[2:3] * lp[3:4], axis=-1, keepdims=True)
    return jnp.exp(s1) - jnp.exp(s2) + lambda_init


def _rms_scale(a, gn, lambda_init):
    ms = jnp.mean(a * a, axis=-1, keepdims=True)
    return a * lax.rsqrt(ms + LN_EPS) * gn * (1.0 - lambda_init)


def _attn_kernel(qb_ref, kb_ref, kj_ref, qi_ref,
                 qd_ref, kd_ref, vd_ref, qf_ref, kf_ref, vf_ref,
                 kdm_ref, vdm_ref, kfm_ref, vfm_ref, lam_ref, gn_ref,
                 yd_ref, yf_ref,
                 md_ref, ld_ref, accd_ref, mf_ref, lf_ref, accf_ref, *, tq, tk, lambda_init):
    p = pl.program_id(0)
    kj = kj_ref[p]
    qi = qi_ref[p]
    n_d = 2 * D_REP
    n_f = F_REP

    def q_diff(g):
        return qd_ref[g * n_d:(g + 1) * n_d].reshape(n_d * tq, LANES)

    def q_fox(g):
        return qf_ref[g * n_f:(g + 1) * n_f].reshape(n_f * tq, LANES)

    def first_block(q, k, v, m_ref, l_ref, acc_ref, g):
        s = _dot_nt(q, k)
        col = lax.broadcasted_iota(jnp.int32, s.shape, 1)
        s = jnp.where(col < N_META, s, NEG_INF)
        m = jnp.max(s, axis=-1, keepdims=True)
        e = jnp.exp2(s - m)
        m_ref[g] = m
        l_ref[g] = jnp.sum(e, axis=-1, keepdims=True)
        acc_ref[g] = _dot(e.astype(bf16), v)

    def block(q, k, v, m_ref, l_ref, acc_ref, g, n_stack, masked):
        s = _dot_nt(q, k)
        if masked:
            row = lax.broadcasted_iota(jnp.int32, (tq, tk), 0)
            col = lax.broadcasted_iota(jnp.int32, (tq, tk), 1)
            s = jnp.where((col <= row)[None], s.reshape(n_stack, tq, tk), NEG_INF).reshape(n_stack * tq, tk)
        m_old = m_ref[g]
        m_new = jnp.maximum(m_old, jnp.max(s, axis=-1, keepdims=True))
        alpha = jnp.exp2(m_old - m_new)
        e = jnp.exp2(s - m_new)
        l_ref[g] = alpha * l_ref[g] + jnp.sum(e, axis=-1, keepdims=True)
        acc_ref[g] = alpha * acc_ref[g] + _dot(e.astype(bf16), v)
        m_ref[g] = m_new

    @pl.when(kj == 0)
    def _():
        for g in range(D_KV_HEADS):
            first_block(q_diff(g), kdm_ref[g], vdm_ref[g], md_ref, ld_ref, accd_ref, g)
        for g in range(F_KV_HEADS):
            first_block(q_fox(g), kfm_ref[g], vfm_ref[g], mf_ref, lf_ref, accf_ref, g)

    def all_blocks(masked):
        for g in range(D_KV_HEADS):
            block(q_diff(g), kd_ref[g], vd_ref[g], md_ref, ld_ref, accd_ref, g, n_d, masked)
        for g in range(F_KV_HEADS):
            block(q_fox(g), kf_ref[g], vf_ref[g], mf_ref, lf_ref, accf_ref, g, n_f, masked)

    @pl.when(kj < qi)
    def _():
        all_blocks(False)

    @pl.when(kj == qi)
    def _():
        all_blocks(True)
        lam = _diff_lambda(lam_ref, lambda_init)
        gn = gn_ref[...]
        for g in range(D_KV_HEADS):
            o = (accd_ref[g] / ld_ref[g]).reshape(n_d, tq, LANES)
            for r in range(D_REP):
                a = o[2 * r] - lam * o[2 * r + 1]
                h = g * D_REP + r
                yd_ref[:, h * LANES:(h + 1) * LANES] = _rms_scale(a, gn, lambda_init).astype(bf16)
        for g in range(F_KV_HEADS):
            o = (accf_ref[g] / lf_ref[g]).reshape(n_f, tq, F_HEAD_DIM)
            for r in range(F_REP):
                h = g * F_REP + r
                yf_ref[:, h * F_HEAD_DIM:(h + 1) * F_HEAD_DIM] = o[r].astype(bf16)


def _attention(qd, kd16, vd16, qfa, kfa, vf16, small, lam_p, gn, *, batch, seq, lambda_init):
    tq = tk = ATTN_TILE if seq % ATTN_TILE == 0 else seq
    nq = seq // tq
    rows = batch * seq
    qb, kb, kjs, qis = [], [], [], []
    for b in range(batch):
        for i in range(nq):
            for j in range(i + 1):
                qb.append(b * nq + i)
                kb.append(b * nq + j)
                kjs.append(j)
                qis.append(i)
    tables = [jnp.asarray(np.asarray(t, np.int32)) for t in (qb, kb, kjs, qis)]
    kdm, vdm, kfm, vfm = small
    qmap = lambda p, qb, kb, kj, qi: (0, qb[p], 0)
    kmap = lambda p, qb, kb, kj, qi: (0, kb[p], 0)
    omap = lambda p, qb, kb, kj, qi: (qb[p], 0)
    zero3 = lambda p, qb, kb, kj, qi: (0, 0, 0)
    zero2 = lambda p, qb, kb, kj, qi: (0, 0)
    n_d = 2 * D_REP
    grid_spec = pltpu.PrefetchScalarGridSpec(
        num_scalar_prefetch=4,
        grid=(len(qb),),
        in_specs=[
            pl.BlockSpec((2 * D_HEADS, tq, LANES), qmap),
            pl.BlockSpec((D_KV_HEADS, tk, LANES), kmap),
            pl.BlockSpec((D_KV_HEADS, tk, LANES), kmap),
            pl.BlockSpec((F_HEADS, tq, LANES), qmap),
            pl.BlockSpec((F_KV_HEADS, tk, LANES), kmap),
            pl.BlockSpec((F_KV_HEADS, tk, F_HEAD_DIM), kmap),
            pl.BlockSpec((D_KV_HEADS, SMALL_TILE, LANES), zero3),
            pl.BlockSpec((D_KV_HEADS, SMALL_TILE, LANES), zero3),
            pl.BlockSpec((F_KV_HEADS, SMALL_TILE, LANES), zero3),
            pl.BlockSpec((F_KV_HEADS, SMALL_TILE, F_HEAD_DIM), zero3),
            pl.BlockSpec(lam_p.shape, zero2),
            pl.BlockSpec(gn.shape, zero2),
        ],
        out_specs=[pl.BlockSpec((tq, D_Q), omap), pl.BlockSpec((tq, F_Q), omap)],
        scratch_shapes=[
            pltpu.VMEM((D_KV_HEADS, n_d * tq, 1), f32), pltpu.VMEM((D_KV_HEADS, n_d * tq, 1), f32),
            pltpu.VMEM((D_KV_HEADS, n_d * tq, LANES), f32),
            pltpu.VMEM((F_KV_HEADS, F_REP * tq, 1), f32), pltpu.VMEM((F_KV_HEADS, F_REP * tq, 1), f32),
            pltpu.VMEM((F_KV_HEADS, F_REP * tq, F_HEAD_DIM), f32),
        ],
    )
    kern = functools.partial(_attn_kernel, tq=tq, tk=tk, lambda_init=lambda_init)
    return pl.pallas_call(
        kern,
        out_shape=[jax.ShapeDtypeStruct((rows, D_Q), bf16), jax.ShapeDtypeStruct((rows, F_Q), bf16)],
        grid_spec=grid_spec,
        compiler_params=pltpu.CompilerParams(dimension_semantics=("arbitrary",),
                                             vmem_limit_bytes=VMEM_LIMIT),
        name="attn",
    )(*tables, qd, kd16, vd16, qfa, kfa, vf16, kdm, vdm, kfm, vfm, lam_p, gn)


def _decode_kernel(pt_ref,
                   qd_ref, qf_ref, kdn_ref, vdn_ref, kfn_ref, vfn_ref, lfn_ref, lam_ref, gn_ref,
                   cdk_ref, cdv_ref, cfk_ref, cfv_ref, clf_ref,
                   yd_ref, yf_ref,
                   kd_buf, vd_buf, kf_buf, vf_buf, lf_buf, sems, *, n_pages, chunk, n_new, lambda_init):
    seq = pl.program_id(0)
    n_chunks = n_pages // chunk
    caches = (cdk_ref, cdv_ref, cfk_ref, cfv_ref, clf_ref)
    bufs = (kd_buf, vd_buf, kf_buf, vf_buf, lf_buf)

    def chunk_copies(ci, slot):
        copies = []
        for c in range(chunk):
            page = pt_ref[seq, n_pages - 1 - (ci * chunk + c)]
            for k, (src, dst) in enumerate(zip(caches, bufs)):
                copies.append(pltpu.make_async_copy(src.at[page], dst.at[slot, c], sems.at[slot, k]))
        return copies

    for cp in chunk_copies(0, 0):
        cp.start()

    qd = qd_ref[0]
    qf = qf_ref[0]
    n_rows = qd.shape[0]
    lane = lax.broadcasted_iota(jnp.int32, (n_rows, LANES), 1)
    row = lax.broadcasted_iota(jnp.int32, (n_rows, LANES), 0)
    lane8 = lax.broadcasted_iota(jnp.int32, (F_HEADS, LANES), 1)

    def softmax_update(state, s_list, v_list):
        m_old, l_old, acc_old = state
        mx = s_list[0]
        for s in s_list[1:]:
            mx = jnp.maximum(mx, s)
        m_new = jnp.maximum(m_old, jnp.max(mx, axis=-1, keepdims=True))
        alpha = jnp.exp2(m_old - m_new)
        e_list = [jnp.exp2(s - m_new) for s in s_list]
        e_sum = e_list[0]
        for e in e_list[1:]:
            e_sum = e_sum + e
        l_new = alpha * l_old + jnp.sum(e_sum, axis=-1, keepdims=True)
        pv = _dot(e_list[0], v_list[0])
        for e, v in zip(e_list[1:], v_list[1:]):
            pv = pv + _dot(e, v)
        return m_new, l_new, alpha * acc_old + pv

    vis_d = lane <= (row % n_new)
    vis_f = lane <= (row // F_HEADS)
    s_d = jnp.where(vis_d, _dot_nt(qd, kdn_ref[0]), NEG_INF)
    x_new = lfn_ref[0]
    cum = x_new
    for sh in (1, 2):
        cum = cum + jnp.where(lane8 >= sh, pltpu.roll(cum, sh, 1), 0.0)
    bias_new = jnp.concatenate([-cum * LOG2E] * n_new, axis=0)
    s_f = jnp.where(vis_f, _dot_nt(qf, kfn_ref[0]) + bias_new, NEG_INF)
    zero_state = (jnp.full((n_rows, 1), NEG_INF, f32), jnp.zeros((n_rows, 1), f32),
                  jnp.zeros((n_rows, 2 * LANES), f32))
    st_d = softmax_update(zero_state, [s_d], [vdn_ref[0]])
    st_f = softmax_update(zero_state, [s_f], [vfn_ref[0]])
    decay0 = jnp.zeros((F_HEADS, 1), f32)

    def body(ci, carry):
        st_d, st_f, decay = carry
        slot = ci % 2

        @pl.when(ci + 1 < n_chunks)
        def _():
            for cp in chunk_copies(ci + 1, 1 - slot):
                cp.start()

        for cp in chunk_copies(ci, slot):
            cp.wait()

        sd_list = [_dot_nt(qd, kd_buf[slot, c]) for c in range(chunk)]
        st_d = softmax_update(st_d, sd_list, [vd_buf[slot, c] for c in range(chunk)])
        sf_list = []
        for c in range(chunk):
            x = lf_buf[slot, c]
            suf = x
            sh = 1
            while sh < LANES:
                suf = suf + jnp.where(lane8 < LANES - sh, pltpu.roll(suf, LANES - sh, 1), 0.0)
                sh *= 2
            bias = (suf - x + decay) * LOG2E
            decay = decay + suf[:, 0:1]
            sf_list.append(_dot_nt(qf, kf_buf[slot, c]) + jnp.concatenate([bias] * n_new, axis=0))
        st_f = softmax_update(st_f, sf_list, [vf_buf[slot, c] for c in range(chunk)])
        return st_d, st_f, decay

    st_d, st_f, _ = lax.fori_loop(0, n_chunks, body, (st_d, st_f, decay0))

    lam = _diff_lambda(lam_ref, lambda_init)
    o = st_d[2] / st_d[1]
    half = n_rows // 2
    a = o[:half] - lam * o[half:]
    per_g = half // D_KV_HEADS
    a = jnp.concatenate([a[g * per_g:(g + 1) * per_g, g * LANES:(g + 1) * LANES]
                         for g in range(D_KV_HEADS)], axis=0)
    yd_ref[0] = _rms_scale(a, gn_ref[...], lambda_init)
    yf_ref[0] = st_f[2] / st_f[1]


def _decode(page_table, qd_s, qf_s, new_pages, lam_p, gn, caches, *, lambda_init):
    n_seq, n_pages = page_table.shape
    chunk = DECODE_CHUNK if n_pages % DECODE_CHUNK == 0 else 1
    kdn, vdn, kfn, vfn, lfn, n_new = new_pages
    cdk, cdv, cfk, cfv, clf = caches
    page = cdk.shape[1]
    n_rows = qd_s.shape[1]
    per = lambda s, pt: (s, 0, 0)
    zero2 = lambda s, pt: (0, 0)
    any_spec = pl.BlockSpec(memory_space=pl.ANY)
    grid_spec = pltpu.PrefetchScalarGridSpec(
        num_scalar_prefetch=1,
        grid=(n_seq,),
        in_specs=[pl.BlockSpec((1, n_rows, 2 * LANES), per), pl.BlockSpec((1, n_rows, 2 * LANES), per),
                  pl.BlockSpec((1, page, 2 * LANES), per), pl.BlockSpec((1, page, 2 * LANES), per),
                  pl.BlockSpec((1, page, 2 * LANES), per), pl.BlockSpec((1, page, 2 * LANES), per),
                  pl.BlockSpec((1, F_HEADS, LANES), per),
                  pl.BlockSpec(lam_p.shape, zero2), pl.BlockSpec(gn.shape, zero2),
                  any_spec, any_spec, any_spec, any_spec, any_spec],
        out_specs=[pl.BlockSpec((1, n_rows // 2, LANES), per), pl.BlockSpec((1, n_rows, 2 * LANES), per)],
        scratch_shapes=[pltpu.VMEM((2, chunk, page, 2 * LANES), f32)] * 4
        + [pltpu.VMEM((2, chunk, F_HEADS, LANES), f32), pltpu.SemaphoreType.DMA((2, 5))],
    )
    kern = functools.partial(_decode_kernel, n_pages=n_pages, chunk=chunk, n_new=n_new,
                             lambda_init=lambda_init)
    return pl.pallas_call(
        kern,
        out_shape=[jax.ShapeDtypeStruct((n_seq, n_rows // 2, LANES), f32),
                   jax.ShapeDtypeStruct((n_seq, n_rows, 2 * LANES), f32)],
        grid_spec=grid_spec,
        compiler_params=pltpu.CompilerParams(dimension_semantics=("arbitrary",),
                                             vmem_limit_bytes=VMEM_LIMIT),
        name="decode",
    )(page_table, qd_s, qf_s, kdn, vdn, kfn, vfn, lfn, lam_p, gn, cdk, cdv, cfk, cfv, clf)


def _merge_ln_kernel(h_ref, yd_ref, yf_ref, wg_ref, wud_ref, wuf_ref, wo_ref, g_ref, b_ref, o_ref, *, alpha):
    h = h_ref[...]
    hb = h.astype(bf16)
    d = h.shape[1]
    ga = _dot(hb, wg_ref[:, :d])
    gb = _dot(hb, wg_ref[:, d:])
    m = jax.nn.sigmoid(ga) * _dot(yd_ref[...], wud_ref[...]) + jax.nn.sigmoid(gb) * _dot(yf_ref[...], wuf_ref[...])
    mix = _dot(m.astype(bf16), wo_ref[...])
    o_ref[...] = _layer_norm(alpha * h + mix, g_ref[...], b_ref[...])


def _merge_ln(h, yd, yf, wg, wud, wuf, wo, g, b, *, alpha, tile):
    rows, d = h.shape
    row = lambda i: (i, 0)
    kern = functools.partial(_merge_ln_kernel, alpha=alpha)
    return pl.pallas_call(
        kern,
        out_shape=jax.ShapeDtypeStruct((rows, d), f32),
        grid=(rows // tile,),
        in_specs=[pl.BlockSpec((tile, d), row), pl.BlockSpec((tile, D_Q), row), pl.BlockSpec((tile, F_Q), row),
                  _const_spec(wg.shape), _const_spec(wud.shape), _const_spec(wuf.shape),
                  _const_spec(wo.shape), _const_spec(g.shape), _const_spec(b.shape)],
        out_specs=pl.BlockSpec((tile, d), row),
        compiler_params=pltpu.CompilerParams(dimension_semantics=("arbitrary",),
                                             vmem_limit_bytes=VMEM_LIMIT),
        name="merge_ln",
    )(h, yd, yf, wg, wud, wuf, wo, g, b)


def _rope_tables(pos):
    half = D_HEAD_DIM // 2
    inv_freq = ROPE_THETA ** (-jnp.arange(half, dtype=f32) / half)
    ang = pos.astype(f32)[:, None] * inv_freq[None, :]
    cos = jnp.cos(ang)
    sin = jnp.sin(ang)
    reps = LANES // D_HEAD_DIM
    return jnp.tile(cos, (1, 2 * reps)), jnp.tile(jnp.concatenate([-sin, sin], axis=1), (1, reps))


def kernel(x_prompt, x_sample, cache_dk, cache_dv, cache_fk, cache_fv, cache_flogf, page_table, meta, ffn1_w_in, ffn1_w_out, ln1_g, ln1_b, w_in, b_f, lam_q1, lam_k1, lam_q2, lam_k2, diff_norm_g, w_up_diff, w_up_fox, w_o, ln2_g, ln2_b, ffn2_w_in, ffn2_w_out, ln3_g, ln3_b):
    depth = w_in.shape[0]
    assert depth == 1, "only the single-layer trunk is implemented"
    batch, seq, d_model = x_prompt.shape
    n_seq, n_new, _ = x_sample.shape
    n_pool, page = cache_dk.shape[1:3]
    past_len = page_table.shape[1] * page
    assert meta.shape[0] == N_META and seq % SMALL_TILE == 0 and (n_seq * n_new) % SMALL_TILE == 0
    alpha = (2.0 * depth) ** 0.25
    lambda_init = 0.8 - 0.6 * math.exp(-0.3 * 0)

    w1i, w1o = ffn1_w_in[0].astype(bf16), ffn1_w_out[0].astype(bf16)
    w2i, w2o = ffn2_w_in[0].astype(bf16), ffn2_w_out[0].astype(bf16)
    w = w_in[0]
    n_mix = D_Q + 2 * D_K + F_Q + 2 * F_K
    w_fl = jnp.pad(w[:, n_mix:n_mix + F_HEADS], ((0, 0), (0, LANES - F_HEADS)))
    w_proj = jnp.concatenate([w[:, :n_mix], w_fl], axis=1).astype(bf16)
    w_gate = w[:, n_mix + F_HEADS:].astype(bf16)
    b_f_row = jnp.pad(b_f[0], (0, LANES - F_HEADS)).reshape(1, LANES)
    wud, wuf, wo = w_up_diff[0].astype(bf16), w_up_fox[0].astype(bf16), w_o[0].astype(bf16)
    lam_p = jnp.stack([lam_q1[0], lam_k1[0], lam_q2[0], lam_k2[0]])
    gn = diff_norm_g[0].reshape(1, 2 * D_HEAD_DIM)
    row1 = lambda v: v[0].reshape(1, d_model)

    tile = ROW_TILE if seq % ROW_TILE == 0 else SMALL_TILE
    xp = x_prompt.reshape(batch * seq, d_model)
    hp = _ffn_ln(xp, w1i, w1o, row1(ln1_g), row1(ln1_b), alpha=alpha, tile=tile)
    cos_p, sin_p = _rope_tables(N_META + jnp.arange(seq))
    (qd, kd32, vd32, kf32, vf32, lf, lfw, kd16, vd16, qf, kf16, vf16) = _inproj(
        hp, cos_p, sin_p, w_proj, b_f_row, tile=tile, table_tiles=seq // tile)
    qfa, kfa = _fox_aug(lfw, qf, kf16, rows=batch * seq, tile=tile, tiles_per_seq=seq // tile, ref_row=None)

    n_small = SMALL_TILE + n_seq * n_new
    xs = jnp.concatenate([jnp.pad(meta.astype(f32), ((0, SMALL_TILE - N_META), (0, 0))),
                          x_sample.reshape(n_seq * n_new, d_model)], axis=0)
    pos_s = jnp.concatenate([jnp.minimum(jnp.arange(SMALL_TILE), N_META - 1),
                             past_len + jnp.tile(jnp.arange(n_new), n_seq)])
    hs = _ffn_ln(xs, w1i, w1o, row1(ln1_g), row1(ln1_b), alpha=alpha, tile=SMALL_TILE)
    cos_s, sin_s = _rope_tables(pos_s)
    (qd_s, kd32_s, vd32_s, kf32_s, vf32_s, lf_s, lfw_s, kd16_s, vd16_s, qf_s, kf16_s, vf16_s) = _inproj(
        hs, cos_s, sin_s, w_proj, b_f_row, tile=SMALL_TILE, table_tiles=n_small // SMALL_TILE)
    _, kfa_m = _fox_aug(lfw_s, qf_s, kf16_s, rows=SMALL_TILE, tile=SMALL_TILE, tiles_per_seq=1,
                        ref_row=N_META - 1)

    yd, yf = _attention(qd, kd16, vd16, qfa, kfa, vf16, (kd16_s, vd16_s, kfa_m, vf16_s), lam_p, gn,
                        batch=batch, seq=seq, lambda_init=lambda_init)
    h2 = _merge_ln(hp, yd, yf, w_gate, wud, wuf, wo, row1(ln2_g), row1(ln2_b), alpha=alpha, tile=tile)
    y_prompt = _ffn_ln(h2, w2i, w2o, row1(ln3_g), row1(ln3_b), alpha=alpha, tile=tile)

    s0 = SMALL_TILE
    n_d = 2 * D_REP
    q5 = qd_s[:, s0:].astype(f32).reshape(D_KV_HEADS, D_REP, 2, n_seq, n_new, LANES)
    q5 = q5.transpose(3, 2, 0, 1, 4, 5)
    eye_d = jnp.eye(D_KV_HEADS, dtype=f32)
    qd_dec = (q5[:, :, :, :, :, None, :] * eye_d[None, None, :, None, None, :, None]).reshape(
        n_seq, 2 * D_KV_HEADS * D_REP * n_new, D_KV_HEADS * LANES)
    qf4 = qf_s[s0:].astype(f32).reshape(n_seq, n_new, F_KV_HEADS, F_REP, F_HEAD_DIM)
    eye_f = jnp.eye(F_KV_HEADS, dtype=f32)
    qf_dec = (qf4[:, :, :, :, None, :] * eye_f[None, None, :, None, :, None]).reshape(
        n_seq, n_new * F_HEADS, F_KV_HEADS * F_HEAD_DIM)
    pad_page = lambda a: jnp.pad(a[s0:].reshape(n_seq, n_new, a.shape[-1]), ((0, 0), (0, page - n_new), (0, 0)))
    lfn = jnp.pad(lf_s[s0:].reshape(n_seq, n_new, F_HEADS).transpose(0, 2, 1), ((0, 0), (0, 0), (0, LANES - n_new)))
    new_pages = (pad_page(kd32_s), pad_page(vd32_s), pad_page(kf32_s), pad_page(vf32_s), lfn, n_new)
    caches = (cache_dk[0].reshape(n_pool, page, D_K), cache_dv[0].reshape(n_pool, page, D_K),
              cache_fk[0].reshape(n_pool, page, F_K), cache_fv[0].reshape(n_pool, page, F_K),
              cache_flogf[0].transpose(0, 2, 1))
    yd_raw, yf_raw = _decode(page_table, qd_dec, qf_dec, new_pages, lam_p, gn, caches, lambda_init=lambda_init)
    yd_s = yd_raw.reshape(n_seq, D_KV_HEADS, D_REP, n_new, LANES).transpose(0, 3, 1, 2, 4)
    yd_s = yd_s.reshape(n_seq * n_new, D_Q).astype(bf16)
    yf_s = yf_raw.reshape(n_seq, n_new, F_KV_HEADS, F_REP, F_KV_HEADS, F_HEAD_DIM)
    yf_s = jnp.einsum("sqgrgd->sqgrd", yf_s).reshape(n_seq * n_new, F_Q).astype(bf16)
    hs_tok = hs[s0:]
    h2s = _merge_ln(hs_tok, yd_s, yf_s, w_gate, wud, wuf, wo, row1(ln2_g), row1(ln2_b), alpha=alpha,
                    tile=SMALL_TILE)
    y_sample = _ffn_ln(h2s, w2i, w2o, row1(ln3_g), row1(ln3_b), alpha=alpha, tile=SMALL_TILE)

    def with_meta(small_arr, prompt_arr, tail):
        m = jnp.broadcast_to(small_arr[None, :N_META], (batch, N_META) + small_arr.shape[1:])
        full = jnp.concatenate([m, prompt_arr.reshape((batch, seq) + prompt_arr.shape[1:])], axis=1)
        return full.reshape((1, batch, N_META + seq) + tail)

    def sample_out(small_arr, tail):
        return small_arr[s0:].reshape((1, n_seq, n_new) + tail)

    dkv = (D_KV_HEADS, 2 * D_HEAD_DIM)
    fkv = (F_KV_HEADS, F_HEAD_DIM)
    return (y_prompt.reshape(batch, seq, d_model), y_sample.reshape(n_seq, n_new, d_model),
            with_meta(kd32_s, kd32, dkv), with_meta(vd32_s, vd32, dkv),
            with_meta(kf32_s, kf32, fkv), with_meta(vf32_s, vf32, fkv), with_meta(lf_s, lf, (F_HEADS,)),
            sample_out(kd32_s, dkv), sample_out(vd32_s, dkv),
            sample_out(kf32_s, fkv), sample_out(vf32_s, fkv), sample_out(lf_s, (F_HEADS,)))
```
